```python
import jax, jax.numpy as jnp
from jax import lax
import numpy as np

D_MODEL = 1024
BATCH = 32
SEQ = 2048
DEPTH = 4

HEAD_DIM = 64
ROPE_THETA = 10000.0
EPS = 1e-6
DIL_GROUPS = ((128, 1), (512, 4), (2048, 16))
DIL_HEADS_PER_GROUP = D_MODEL // 256
DIL_HEADS = len(DIL_GROUPS) * DIL_HEADS_PER_GROUP
DIL_WIDTH = DIL_HEADS * HEAD_DIM
DIL_OUT_WIDTH = DIL_HEADS_PER_GROUP * HEAD_DIM
CONV_CH = D_MODEL // 2
CONV_WIDTH = 31
DIFF_HEADS = D_MODEL // 256
DIFF_WIDTH = DIFF_HEADS * 2 * HEAD_DIM
ATTN_BLOCK = 128
D_FF = ((8 * D_MODEL // 3 + 127) // 128) * 128
N_BRANCH = 3
IN_SPLIT_SIZES = (DIL_WIDTH, DIL_WIDTH, DIL_WIDTH, DIFF_WIDTH, DIFF_WIDTH, DIFF_WIDTH,
                  2 * CONV_CH, N_BRANCH * D_MODEL)
IN_WIDTH = sum(IN_SPLIT_SIZES)

kernel_name = 'hybrid_gated_dilated_conv_diffattn_macaron'


def _rmsnorm(x, g):
    xf = x.astype(jnp.float32)
    y = xf * lax.rsqrt(jnp.mean(xf * xf, axis=-1, keepdims=True) + EPS)
    return (y * g.astype(jnp.float32)).astype(x.dtype)


def _layernorm(x, g, b):
    xf = x.astype(jnp.float32)
    mu = jnp.mean(xf, axis=-1, keepdims=True)
    var = jnp.mean(jnp.square(xf - mu), axis=-1, keepdims=True)
    y = (xf - mu) * lax.rsqrt(var + EPS)
    return (y * g.astype(jnp.float32) + b.astype(jnp.float32)).astype(x.dtype)


def _swiglu(h, wg, wu, wd):
    return (jax.nn.silu(h @ wg) * (h @ wu)) @ wd


def _rope_tables(seq, dim):
    inv = ROPE_THETA ** (-jnp.arange(0, dim, 2, dtype=jnp.float32) / dim)
    ang = jnp.arange(seq, dtype=jnp.float32)[:, None] * inv[None, :]
    return jnp.cos(ang), jnp.sin(ang)


def _apply_rope(x, cos, sin):
    x1, x2 = jnp.split(x.astype(jnp.float32), 2, axis=-1)
    c, s = cos[:, None, :], sin[:, None, :]
    return jnp.concatenate([x1 * c - x2 * s, x2 * c + x1 * s], axis=-1).astype(x.dtype)


def _banded_window_attention(q, k, v, back):
    n, length, h, hd = q.shape
    blk = back
    nb = -(-length // blk)
    pad = nb * blk - length
    padw = ((0, 0), (0, pad), (0, 0), (0, 0))
    q, k, v = (jnp.pad(t, padw) for t in (q, k, v))
    qb = q.reshape(n, nb, blk, h, hd)
    kb = k.reshape(n, nb, blk, h, hd)
    vb = v.reshape(n, nb, blk, h, hd)
    prev = lambda t: jnp.pad(t, ((0, 0), (1, 0), (0, 0), (0, 0), (0, 0)))[:, :nb]
    kc = jnp.concatenate([prev(kb), kb], axis=2)
    vc = jnp.concatenate([prev(vb), vb], axis=2)
    s = jnp.einsum('nbqhd,nbkhd->nbhqk', qb, kc,
                   preferred_element_type=jnp.float32) * (hd ** -0.5)
    qpos = jnp.arange(blk)[:, None] + blk
    kpos = jnp.arange(2 * blk)[None, :]
    rel = qpos - kpos
    band = (rel >= 0) & (rel <= back)
    has_prev = (jnp.arange(nb) > 0)[:, None, None] | (kpos >= blk)[None]
    mask = band[None] & has_prev
    s = jnp.where(mask[None, :, None], s, -jnp.inf)
    m = jnp.max(s, axis=-1, keepdims=True)
    p = jnp.exp(s - m)
    denom = jnp.sum(p, axis=-1)
    o = jnp.einsum('nbhqk,nbkhd->nbqhd', p, vc.astype(jnp.float32))
    o = o / jnp.moveaxis(denom, 2, 3)[..., None]
    lse = jnp.moveaxis(m[..., 0] + jnp.log(denom), 2, 3)
    o = o.reshape(n, nb * blk, h, hd)[:, :length]
    lse = lse.reshape(n, nb * blk, h)[:, :length]
    return o, lse


def _dilated_group(q, k, v, window, dilation):
    b, s, h, hd = q.shape
    sub_len = s // dilation
    def split_phase(t):
        return t.reshape(b, sub_len, dilation, h, hd).transpose(0, 2, 1, 3, 4).reshape(
            b * dilation, sub_len, h, hd)
    o, lse = _banded_window_attention(split_phase(q), split_phase(k), split_phase(v),
                                      window // dilation)
    o = o.reshape(b, dilation, sub_len, h, hd).transpose(0, 2, 1, 3, 4).reshape(b, s, h, hd)
    lse = lse.reshape(b, dilation, sub_len, h).transpose(0, 2, 1, 3).reshape(b, s, h)
    return o, lse


def _dilated_attention(q, k, v):
    b, s = q.shape[:2]
    outs, lses = [], []
    for g, (window, dilation) in enumerate(DIL_GROUPS):
        hs = slice(g * DIL_HEADS_PER_GROUP, (g + 1) * DIL_HEADS_PER_GROUP)
        o, lse = _dilated_group(q[:, :, hs], k[:, :, hs], v[:, :, hs], window, dilation)
        outs.append(o)
        lses.append(lse)
    wts = jax.nn.softmax(jnp.stack(lses, axis=0), axis=0)
    o = jnp.sum(wts[..., None] * jnp.stack(outs, axis=0), axis=0)
    return o.reshape(b, s, DIL_OUT_WIDTH)


def _conformer_conv(u, conv_w, conv_b, norm_g, norm_b):
    a, gt = jnp.split(u, 2, axis=-1)
    z = a * jax.nn.sigmoid(gt)
    zp = jnp.pad(z, ((0, 0), (CONV_WIDTH - 1, 0), (0, 0)))
    y = lax.conv_general_dilated(zp, conv_w[:, None, :], window_strides=(1,), padding='VALID',
                                 dimension_numbers=('NWC', 'WIO', 'NWC'),
                                 feature_group_count=CONV_CH) + conv_b
    return jax.nn.silu(_layernorm(y, norm_g, norm_b))


def _lambda_init(layer):
    return 0.8 - 0.6 * float(np.exp(-0.3 * layer))


def _diff_attention(q, k, v, lam, subln_g, lambda_init):
    b, s = q.shape[:2]
    scale = HEAD_DIM ** -0.5
    outs = []
    for i in range(s // ATTN_BLOCK):
        lo, hi = i * ATTN_BLOCK, (i + 1) * ATTN_BLOCK
        sc = jnp.einsum('bqhcd,bkhcd->bhcqk', q[:, lo:hi], k[:, :hi],
                        preferred_element_type=jnp.float32) * scale
        causal = (lo + jnp.arange(ATTN_BLOCK))[:, None] >= jnp.arange(hi)[None, :]
        p = jax.nn.softmax(jnp.where(causal, sc, -jnp.inf), axis=-1)
        a = p[:, :, 0] - lam * p[:, :, 1]
        outs.append(jnp.einsum('bhqk,bkhe->bqhe', a, v[:, :hi].astype(jnp.float32)))
    o = jnp.concatenate(outs, axis=1)
    o = _rmsnorm(o, subln_g) * (1.0 - lambda_init)
    return o.reshape(b, s, DIFF_WIDTH)


def setup_inputs(seed: int = 0) -> dict:
    key = jax.random.key(seed)
    ks = iter(jax.random.split(key, 32))
    nrm = lambda shape, scale: scale * jax.random.normal(next(ks), shape, jnp.float32)
    gain = lambda shape: 1.0 + nrm(shape, 0.02)
    L = DEPTH
    return {
        'x': nrm((BATCH, SEQ, D_MODEL), 1.0),
        'ffn1_norm_pre': gain((L, D_MODEL)),
        'ffn1_norm_post': gain((L, D_MODEL)),
        'ffn1_w_gate': nrm((L, D_MODEL, D_FF), D_MODEL ** -0.5),
        'ffn1_w_up': nrm((L, D_MODEL, D_FF), D_MODEL ** -0.5),
        'ffn1_w_down': nrm((L, D_FF, D_MODEL), D_FF ** -0.5),
        'mix_norm_pre': gain((L, D_MODEL)),
        'mix_norm_post': gain((L, D_MODEL)),
        'w_in': nrm((L, D_MODEL, IN_WIDTH), D_MODEL ** -0.5),
        'b_gate': nrm((L, N_BRANCH * D_MODEL), 0.1),
        'conv_w': nrm((L, CONV_WIDTH, CONV_CH), CONV_WIDTH ** -0.5),
        'conv_b': nrm((L, CONV_CH), 0.02),
        'conv_norm_g': gain((L, CONV_CH)),
        'conv_norm_b': nrm((L, CONV_CH), 0.02),
        'lambda_q1': nrm((L, HEAD_DIM), 0.1),
        'lambda_k1': nrm((L, HEAD_DIM), 0.1),
        'lambda_q2': nrm((L, HEAD_DIM), 0.1),
        'lambda_k2': nrm((L, HEAD_DIM), 0.1),
        'diff_subln': gain((L, 2 * HEAD_DIM)),
        'w_proj_a': nrm((L, DIL_OUT_WIDTH, D_MODEL), DIL_OUT_WIDTH ** -0.5),
        'w_proj_b': nrm((L, CONV_CH, D_MODEL), CONV_CH ** -0.5),
        'w_proj_c': nrm((L, DIFF_WIDTH, D_MODEL), DIFF_WIDTH ** -0.5),
        'w_out': nrm((L, D_MODEL, D_MODEL), D_MODEL ** -0.5),
        'ffn2_norm_pre': gain((L, D_MODEL)),
        'ffn2_norm_post': gain((L, D_MODEL)),
        'ffn2_w_gate': nrm((L, D_MODEL, D_FF), D_MODEL ** -0.5),
        'ffn2_w_up': nrm((L, D_MODEL, D_FF), D_MODEL ** -0.5),
        'ffn2_w_down': nrm((L, D_FF, D_MODEL), D_FF ** -0.5),
    }


def reference(x, ffn1_norm_pre, ffn1_norm_post, ffn1_w_gate, ffn1_w_up, ffn1_w_down,
              mix_norm_pre, mix_norm_post, w_in, b_gate, conv_w, conv_b, conv_norm_g, conv_norm_b,
              lambda_q1, lambda_k1, lambda_q2, lambda_k2, diff_subln,
              w_proj_a, w_proj_b, w_proj_c, w_out,
              ffn2_norm_pre, ffn2_norm_post, ffn2_w_gate, ffn2_w_up, ffn2_w_down):
    b, s, _ = x.shape
    cos, sin = _rope_tables(s, HEAD_DIM)
    offsets = [int(o) for o in np.cumsum(IN_SPLIT_SIZES)[:-1]]
    f32 = jnp.float32
    for l in range(DEPTH):
        h = _rmsnorm(x, ffn1_norm_pre[l])
        x = x + 0.5 * _rmsnorm(_swiglu(h, ffn1_w_gate[l], ffn1_w_up[l], ffn1_w_down[l]),
                               ffn1_norm_post[l])

        h = _rmsnorm(x, mix_norm_pre[l])
        proj = h @ w_in[l]
        qa, ka, va, qc, kc, vc, conv_in, gate_pre = jnp.split(proj, offsets, axis=-1)

        qa = _apply_rope(qa.reshape(b, s, DIL_HEADS, HEAD_DIM), cos, sin)
        ka = _apply_rope(ka.reshape(b, s, DIL_HEADS, HEAD_DIM), cos, sin)
        va = va.reshape(b, s, DIL_HEADS, HEAD_DIM)
        oa = _dilated_attention(qa, ka, va).astype(x.dtype)

        ob = _conformer_conv(conv_in, conv_w[l], conv_b[l], conv_norm_g[l], conv_norm_b[l])

        lam_init = _lambda_init(l)
        lam = (jnp.exp(jnp.sum(lambda_q1[l].astype(f32) * lambda_k1[l].astype(f32)))
               - jnp.exp(jnp.sum(lambda_q2[l].astype(f32) * lambda_k2[l].astype(f32)))
               + lam_init)
        qc = _apply_rope(qc.reshape(b, s, 2 * DIFF_HEADS, HEAD_DIM), cos, sin).reshape(
            b, s, DIFF_HEADS, 2, HEAD_DIM)
        kc = _apply_rope(kc.reshape(b, s, 2 * DIFF_HEADS, HEAD_DIM), cos, sin).reshape(
            b, s, DIFF_HEADS, 2, HEAD_DIM)
        vc = vc.reshape(b, s, DIFF_HEADS, 2 * HEAD_DIM)
        oc = _diff_attention(qc, kc, vc, lam, diff_subln[l], lam_init).astype(x.dtype)

        ya = oa @ w_proj_a[l]
        yb = ob @ w_proj_b[l]
        yc = oc @ w_proj_c[l]
        gates = jax.nn.sigmoid(gate_pre + b_gate[l]).reshape(b, s, N_BRANCH, D_MODEL)
        merged = gates[:, :, 0] * ya + gates[:, :, 1] * yb + gates[:, :, 2] * yc
        x = x + _rmsnorm(merged @ w_out[l], mix_norm_post[l])

        h = _rmsnorm(x, ffn2_norm_pre[l])
        x = x + 0.5 * _rmsnorm(_swiglu(h, ffn2_w_gate[l], ffn2_w_up[l], ffn2_w_down[l]),
                               ffn2_norm_post[l])
    return x
```

```python
import functools
import math

import numpy as np
import jax
import jax.numpy as jnp
from jax import lax
from jax.experimental import pallas as pl
from jax.experimental.pallas import tpu as pltpu

F32 = jnp.float32
BF16 = jnp.bfloat16

D_MODEL = 1024
SEQ = 2048
DEPTH = 4
HEAD_DIM = 64
ROPE_THETA = 10000.0
EPS = 1e-6
DIL_GROUPS = ((128, 1), (512, 4), (2048, 16))
N_GROUPS = len(DIL_GROUPS)
GROUP_W = 256
DIL_WIDTH = N_GROUPS * GROUP_W
CONV_CH = 512
CONV_WIDTH = 31
DIFF_HEADS = 4
DIFF_WIDTH = 512
D_FF = 2816
N_BRANCH = 3
PROJ_W = 3 * DIL_WIDTH + 3 * DIFF_WIDTH + 2 * CONV_CH
GATE_W = N_BRANCH * D_MODEL

LANES = 128
ATT_BLK = 128
DIFF_BLK = 256
TOK_TILE = 512
CONV_HALO = 32
NEG = -1e30
V7X_VMEM_LIMIT = 56 * 1024 * 1024


def _cparams(n_grid):
    return pltpu.CompilerParams(dimension_semantics=("arbitrary",) * n_grid,
                                vmem_limit_bytes=V7X_VMEM_LIMIT)


def _resident(shape):
    nd = len(shape)
    return pl.BlockSpec(shape, lambda *_: (0,) * nd, pipeline_mode=pl.Buffered(1))


def _rms(x, g):
    return x * lax.rsqrt(jnp.mean(x * x, axis=-1, keepdims=True) + EPS) * g


def _sigmoid(x):
    return 1.0 / (1.0 + jnp.exp(-x))


def _dot(a, b):
    return jnp.dot(a, b, preferred_element_type=F32)


def _dot_nt(a, b):
    return lax.dot_general(a, b, (((1,), (1,)), ((), ())), preferred_element_type=F32)


def _ffn_body(x_ref, gpre_ref, gpost_ref, wg_ref, wu_ref, wd_ref, o_ref):
    x = x_ref[...]
    h = _rms(x, gpre_ref[...]).astype(BF16)
    g = _dot(h, wg_ref[...])
    u = _dot(h, wu_ref[...])
    a = (g * _sigmoid(g) * u).astype(BF16)
    y = _dot(a, wd_ref[...])
    o_ref[...] = x + 0.5 * _rms(y, gpost_ref[...])


def _ffn(x2, gpre, gpost, wg, wu, wd):
    n = x2.shape[0]
    tok = pl.BlockSpec((TOK_TILE, D_MODEL), lambda i: (i, 0))
    return pl.pallas_call(
        _ffn_body,
        grid=(n // TOK_TILE,),
        in_specs=[tok, _resident((1, D_MODEL)), _resident((1, D_MODEL)),
                  _resident((D_MODEL, D_FF)), _resident((D_MODEL, D_FF)), _resident((D_FF, D_MODEL))],
        out_specs=tok,
        out_shape=jax.ShapeDtypeStruct((n, D_MODEL), F32),
        compiler_params=_cparams(1),
        name="ffn",
    )(x2, gpre, gpost, wg, wu, wd)


def _rope(r, ct, st):
    first_half = (lax.broadcasted_iota(jnp.int32, (1, LANES), 1) % HEAD_DIM) < (HEAD_DIM // 2)
    outs = []
    for j in range(r.shape[1] // LANES):
        xb = r[:, j * LANES:(j + 1) * LANES]
        rot = jnp.where(first_half,
                        pltpu.roll(xb, LANES - HEAD_DIM // 2, 1),
                        pltpu.roll(xb, HEAD_DIM // 2, 1))
        outs.append(xb * ct + rot * st)
    return outs[0] if len(outs) == 1 else jnp.concatenate(outs, axis=1)


def _proj_body(x_ref, g_ref, w_ref, cq_ref, sq_ref, ck_ref, sk_ref,
               a0_ref, a1_ref, a2_ref, c_ref, z_ref, scr_ref):
    h = _rms(x_ref[...], g_ref[...]).astype(BF16)
    tables = ((cq_ref[...], sq_ref[...]), (ck_ref[...], sk_ref[...]))
    a_refs = (a0_ref, a1_ref, a2_ref)
    for part in range(3):
        for grp, (_, dil) in enumerate(DIL_GROUPS):
            c0 = part * DIL_WIDTH + grp * GROUP_W
            r = _dot(h, w_ref[:, c0:c0 + GROUP_W])
            if part < 2:
                r = _rope(r, *tables[part])
            oc0 = part * GROUP_W
            if dil == 1:
                a_refs[grp][0, :, oc0:oc0 + GROUP_W] = r.astype(BF16)
                continue
            rows = TOK_TILE // dil
            for slab in range(GROUP_W // LANES):
                si = ((grp - 1) * 3 + part) * (GROUP_W // LANES) + slab
                scr_ref[si] = r[:, slab * LANES:(slab + 1) * LANES]
                for res in range(dil):
                    a_refs[grp][0, res, :, oc0 + slab * LANES:oc0 + (slab + 1) * LANES] = (
                        scr_ref[si, pl.ds(res, rows, stride=dil), :].astype(BF16))
    base = 3 * DIL_WIDTH
    for part in range(3):
        r = _dot(h, w_ref[:, base + part * DIFF_WIDTH:base + (part + 1) * DIFF_WIDTH])
        if part < 2:
            r = _rope(r, *tables[part])
        c_ref[0, :, part * DIFF_WIDTH:(part + 1) * DIFF_WIDTH] = r.astype(BF16)
    base += 3 * DIFF_WIDTH
    r = _dot(h, w_ref[:, base:base + 2 * CONV_CH])
    z_ref[0] = r[:, :CONV_CH] * _sigmoid(r[:, CONV_CH:])


def _proj(x2, g, w, rope_tabs, batch):
    n = x2.shape[0]
    tps = SEQ // TOK_TILE
    tab = pl.BlockSpec((TOK_TILE, LANES), lambda i: (i % tps, 0))
    n_scr = 2 * 3 * (GROUP_W // LANES)
    return pl.pallas_call(
        _proj_body,
        grid=(n // TOK_TILE,),
        in_specs=[pl.BlockSpec((TOK_TILE, D_MODEL), lambda i: (i, 0)),
                  _resident((1, D_MODEL)), _resident((D_MODEL, PROJ_W)), tab, tab, tab, tab],
        out_specs=[
            pl.BlockSpec((1, TOK_TILE, 3 * GROUP_W), lambda i: (i // tps, i % tps, 0)),
            pl.BlockSpec((1, 4, TOK_TILE // 4, 3 * GROUP_W), lambda i: (i // tps, 0, i % tps, 0)),
            pl.BlockSpec((1, 16, TOK_TILE // 16, 3 * GROUP_W), lambda i: (i // tps, 0, i % tps, 0)),
            pl.BlockSpec((1, TOK_TILE, 3 * DIFF_WIDTH), lambda i: (i // tps, i % tps, 0)),
            pl.BlockSpec((1, TOK_TILE, CONV_CH), lambda i: (i // tps, i % tps, 0)),
        ],
        out_shape=[
            jax.ShapeDtypeStruct((batch, SEQ, 3 * GROUP_W), BF16),
            jax.ShapeDtypeStruct((batch, 4, SEQ // 4, 3 * GROUP_W), BF16),
            jax.ShapeDtypeStruct((batch, 16, SEQ // 16, 3 * GROUP_W), BF16),
            jax.ShapeDtypeStruct((batch, SEQ, 3 * DIFF_WIDTH), BF16),
            jax.ShapeDtypeStruct((batch, SEQ, CONV_CH), F32),
        ],
        scratch_shapes=[pltpu.VMEM((n_scr, TOK_TILE, LANES), F32)],
        compiler_params=_cparams(1),
        name="in_proj",
    )(x2, g, w, *rope_tabs)


def _dil_block(q, kcat, vcat, mask):
    nk = kcat.shape[0]
    n_heads = GROUP_W // HEAD_DIM
    lane_head = lax.broadcasted_iota(jnp.int32, (1, GROUP_W), 1) // HEAD_DIM
    hm_f = [jnp.where(lane_head == h, 1.0, 0.0) for h in range(n_heads)]
    hm = [m.astype(BF16) for m in hm_f]
    kbd = jnp.concatenate([kcat * hm[h] for h in range(n_heads)], axis=0)
    s = _dot_nt(q, kbd)
    ps = []
    m_b = jnp.zeros((q.shape[0], GROUP_W), F32)
    for h in range(n_heads):
        sh = jnp.where(mask, s[:, h * nk:(h + 1) * nk], NEG)
        mh = jnp.max(sh, axis=1, keepdims=True)
        ps.append(jnp.exp(sh - mh).astype(BF16))
        m_b = m_b + mh * hm_f[h]
    p = jnp.concatenate(ps, axis=1)
    vbd = jnp.concatenate([vcat * hm[h] for h in range(n_heads)], axis=0)
    obd = jnp.concatenate([jnp.broadcast_to(hm[h], (nk, GROUP_W)) for h in range(n_heads)], axis=0)
    r = _dot(p, jnp.concatenate([vbd, obd], axis=1))
    denom = r[:, GROUP_W:]
    return r[:, :GROUP_W] / denom, m_b + jnp.log(denom)


def _dil_attn_body(a0_ref, a1_ref, a2_ref, o_ref, acc_ref, lse_ref):
    n_blk = SEQ // ATT_BLK
    qi = lax.broadcasted_iota(jnp.int32, (ATT_BLK, 2 * ATT_BLK), 0)
    kj = lax.broadcasted_iota(jnp.int32, (ATT_BLK, 2 * ATT_BLK), 1)
    cur_ok = (kj >= ATT_BLK) & (kj - ATT_BLK <= qi)
    prev_ok = (kj < ATT_BLK) & (kj >= qi)
    diag_only = (lax.broadcasted_iota(jnp.int32, (ATT_BLK, ATT_BLK), 1)
                 <= lax.broadcasted_iota(jnp.int32, (ATT_BLK, ATT_BLK), 0))

    def store(grp, start, dil, o, lse):
        for slab in range(GROUP_W // LANES):
            sl = slice(slab * LANES, (slab + 1) * LANES)
            if dil == 1:
                idx = pl.ds(start, ATT_BLK)
            else:
                idx = pl.ds(start, ATT_BLK, stride=dil)
            acc_ref[grp, slab, idx, :] = o[:, sl]
            lse_ref[grp, slab, idx, :] = lse[:, sl]

    def blk0(n, carry):
        cur = pl.multiple_of(n * ATT_BLK, ATT_BLK)
        prv = pl.multiple_of(jnp.maximum(n - 1, 0) * ATT_BLK, ATT_BLK)
        q = a0_ref[0, pl.ds(cur, ATT_BLK), 0:GROUP_W]
        kcat = jnp.concatenate([a0_ref[0, pl.ds(prv, ATT_BLK), GROUP_W:2 * GROUP_W],
                                a0_ref[0, pl.ds(cur, ATT_BLK), GROUP_W:2 * GROUP_W]], axis=0)
        vcat = jnp.concatenate([a0_ref[0, pl.ds(prv, ATT_BLK), 2 * GROUP_W:3 * GROUP_W],
                                a0_ref[0, pl.ds(cur, ATT_BLK), 2 * GROUP_W:3 * GROUP_W]], axis=0)
        mask = cur_ok | (prev_ok & (n > 0))
        o, lse = _dil_block(q, kcat, vcat, mask)
        store(0, cur, 1, o, lse)
        return carry

    lax.fori_loop(0, n_blk, blk0, 0)

    sub_blk = (SEQ // 4) // ATT_BLK

    def blk1(n, carry):
        res = n // sub_blk
        jb = n % sub_blk
        cur = pl.multiple_of(jb * ATT_BLK, ATT_BLK)
        prv = pl.multiple_of(jnp.maximum(jb - 1, 0) * ATT_BLK, ATT_BLK)
        q = a1_ref[0, res, pl.ds(cur, ATT_BLK), 0:GROUP_W]
        kcat = jnp.concatenate([a1_ref[0, res, pl.ds(prv, ATT_BLK), GROUP_W:2 * GROUP_W],
                                a1_ref[0, res, pl.ds(cur, ATT_BLK), GROUP_W:2 * GROUP_W]], axis=0)
        vcat = jnp.concatenate([a1_ref[0, res, pl.ds(prv, ATT_BLK), 2 * GROUP_W:3 * GROUP_W],
                                a1_ref[0, res, pl.ds(cur, ATT_BLK), 2 * GROUP_W:3 * GROUP_W]], axis=0)
        mask = cur_ok | (prev_ok & (jb > 0))
        o, lse = _dil_block(q, kcat, vcat, mask)
        store(1, jb * (ATT_BLK * 4) + res, 4, o, lse)
        return carry

    lax.fori_loop(0, n_blk, blk1, 0)

    def blk2(n, carry):
        q = a2_ref[0, n, :, 0:GROUP_W]
        k = a2_ref[0, n, :, GROUP_W:2 * GROUP_W]
        v = a2_ref[0, n, :, 2 * GROUP_W:3 * GROUP_W]
        o, lse = _dil_block(q, k, v, diag_only)
        store(2, n, 16, o, lse)
        return carry

    lax.fori_loop(0, n_blk, blk2, 0)

    rows_per = 256

    def combine(ci, carry):
        rows = pl.ds(pl.multiple_of(ci * rows_per, rows_per), rows_per)
        for slab in range(GROUP_W // LANES):
            ls = [lse_ref[g, slab, rows, :] for g in range(N_GROUPS)]
            mx = jnp.maximum(jnp.maximum(ls[0], ls[1]), ls[2])
            ws = [jnp.exp(l - mx) for l in ls]
            num = ws[0] * acc_ref[0, slab, rows, :]
            for g in range(1, N_GROUPS):
                num = num + ws[g] * acc_ref[g, slab, rows, :]
            o_ref[0, rows, slab * LANES:(slab + 1) * LANES] = (num / (ws[0] + ws[1] + ws[2])).astype(BF16)
        return carry

    lax.fori_loop(0, SEQ // rows_per, combine, 0)


def _dil_attn(a0, a1, a2):
    batch = a0.shape[0]
    return pl.pallas_call(
        _dil_attn_body,
        grid=(batch,),
        in_specs=[pl.BlockSpec((1, SEQ, 3 * GROUP_W), lambda b: (b, 0, 0)),
                  pl.BlockSpec((1, 4, SEQ // 4, 3 * GROUP_W), lambda b: (b, 0, 0, 0)),
                  pl.BlockSpec((1, 16, SEQ // 16, 3 * GROUP_W), lambda b: (b, 0, 0, 0))],
        out_specs=pl.BlockSpec((1, SEQ, GROUP_W), lambda b: (b, 0, 0)),
        out_shape=jax.ShapeDtypeStruct((batch, SEQ, GROUP_W), BF16),
        scratch_shapes=[pltpu.VMEM((N_GROUPS, GROUP_W // LANES, SEQ, LANES), F32),
                        pltpu.VMEM((N_GROUPS, GROUP_W // LANES, SEQ, LANES), F32)],
        compiler_params=_cparams(1),
        name="dil_attn",
    )(a0, a1, a2)


def _diff_attn_body(c_ref, lam_ref, subln_ref, o_ref, q_scr, acc_ref):
    hw = 2 * HEAD_DIM
    lam = (jnp.exp(jnp.sum(lam_ref[0:1, :] * lam_ref[1:2, :], axis=1, keepdims=True))
           - jnp.exp(jnp.sum(lam_ref[2:3, :] * lam_ref[3:4, :], axis=1, keepdims=True))
           + lam_ref[4:5, 0:1])
    out_scale = 1.0 - lam_ref[4:5, 0:1]
    lane = lax.broadcasted_iota(jnp.int32, (1, hw), 1)
    map1 = jnp.where(lane < HEAD_DIM, 1.0, 0.0).astype(BF16)
    map2 = jnp.where(lane >= HEAD_DIM, 1.0, 0.0).astype(BF16)
    causal =(lax.broadcasted_iota(jnp.int32, (DIFF_BLK, DIFF_BLK), 1)
              <= lax.broadcasted_iota(jnp.int32, (DIFF_BLK, DIFF_BLK), 0))
    ones = jnp.ones((DIFF_BLK, hw), BF16)

    def q_block(qi, carry):
        qrows = pl.ds(pl.multiple_of(qi * DIFF_BLK, DIFF_BLK), DIFF_BLK)
        for h in range(DIFF_HEADS):
            q = c_ref[0, qrows, h * hw:(h + 1) * hw]
            q_scr[2 * h] = q * map1
            q_scr[2 * h + 1] = q * map2
        acc_ref[...] = jnp.zeros_like(acc_ref)

        def kv_step(kb, ms, masked):
            krows = pl.ds(pl.multiple_of(kb * DIFF_BLK, DIFF_BLK), DIFF_BLK)
            new_ms = []
            for h in range(DIFF_HEADS):
                k = c_ref[0, krows, DIFF_WIDTH + h * hw:DIFF_WIDTH + (h + 1) * hw]
                v = c_ref[0, krows, 2 * DIFF_WIDTH + h * hw:2 * DIFF_WIDTH + (h + 1) * hw]
                s1 = _dot_nt(q_scr[2 * h], k)
                s2 = _dot_nt(q_scr[2 * h + 1], k)
                if masked:
                    s1 = jnp.where(causal, s1, NEG)
                    s2 = jnp.where(causal, s2, NEG)
                s = jnp.concatenate([s1, s2], axis=0)
                m_new = jnp.maximum(ms[h], jnp.max(s, axis=1, keepdims=True))
                alpha = jnp.exp(ms[h] - m_new)
                p = jnp.exp(s - m_new).astype(BF16)
                pv = _dot(p, jnp.concatenate([v, ones], axis=1))
                acc_ref[h] = acc_ref[h] * alpha + pv
                new_ms.append(m_new)
            return tuple(new_ms)

        ms0 = tuple(jnp.full((2 * DIFF_BLK, 1), NEG, F32) for _ in range(DIFF_HEADS))
        ms = lax.fori_loop(0, qi, lambda kb, ms: kv_step(kb, ms, False), ms0)
        kv_step(qi, ms, True)

        for h in range(DIFF_HEADS):
            a = acc_ref[h]
            o1 = a[:DIFF_BLK, :hw] / a[:DIFF_BLK, hw:]
            o2 = a[DIFF_BLK:, :hw] / a[DIFF_BLK:, hw:]
            o = o1 - lam * o2
            o = _rms(o, subln_ref[...]) * out_scale
            o_ref[0, qrows, h * hw:(h + 1) * hw] = o.astype(BF16)
        return carry

    lax.fori_loop(0, SEQ // DIFF_BLK, q_block, 0)


def _diff_attn(c, lam_tab, subln):
    batch = c.shape[0]
    hw = 2 * HEAD_DIM
    return pl.pallas_call(
        _diff_attn_body,
        grid=(batch,),
        in_specs=[pl.BlockSpec((1, SEQ, 3 * DIFF_WIDTH), lambda b: (b, 0, 0)),
                  _resident((8, HEAD_DIM)), _resident((1, hw))],
        out_specs=pl.BlockSpec((1, SEQ, DIFF_WIDTH), lambda b: (b, 0, 0)),
        out_shape=jax.ShapeDtypeStruct((batch, SEQ, DIFF_WIDTH), BF16),
        scratch_shapes=[pltpu.VMEM((2 * DIFF_HEADS, DIFF_BLK, hw), BF16),
                        pltpu.VMEM((DIFF_HEADS, 2 * DIFF_BLK, 2 * hw), F32)],
        compiler_params=_cparams(1),
        name="diff_attn",
    )(c, lam_tab, subln)


def _merge_body(x_ref, z_ref, zh_ref, oa_ref, oc_ref, gpre_ref, gpost_ref, wgate_ref, bgate_ref,
                cw_ref, cb_ref, lng_ref, lnb_ref, wpa_ref, wpb_ref, wpc_ref, wout_ref,
                o_ref, zp_ref):
    i = pl.program_id(0)
    x = x_ref[...]
    h = _rms(x, gpre_ref[...]).astype(BF16)

    first = (i % (SEQ // TOK_TILE)) == 0
    zp_ref[0:CONV_HALO, :] = jnp.where(first, 0.0, zh_ref[...])
    zp_ref[CONV_HALO:, :] = z_ref[...]
    off = CONV_HALO - (CONV_WIDTH - 1)
    y = jnp.zeros((TOK_TILE, CONV_CH), F32) + cb_ref[...]
    for k in range(CONV_WIDTH):
        y = y + zp_ref[off + k:off + k + TOK_TILE, :] * cw_ref[k:k + 1, :]
    mu = jnp.mean(y, axis=-1, keepdims=True)
    yc = y - mu
    var = jnp.mean(yc * yc, axis=-1, keepdims=True)
    yn = yc * lax.rsqrt(var + EPS) * lng_ref[...] + lnb_ref[...]
    ob = (yn * _sigmoid(yn)).astype(BF16)

    branches = ((oa_ref[...], wpa_ref), (ob, wpb_ref), (oc_ref[...], wpc_ref))
    merged = None
    for j, (act, w_ref) in enumerate(branches):
        gate = _sigmoid(_dot(h, wgate_ref[:, j * D_MODEL:(j + 1) * D_MODEL])
                        + bgate_ref[:, j * D_MODEL:(j + 1) * D_MODEL])
        term = gate * _dot(act, w_ref[...])
        merged = term if merged is None else merged + term
    y_out = _dot(merged.astype(BF16), wout_ref[...])
    o_ref[...] = x + _rms(y_out, gpost_ref[...])


def _merge(x2, z2, oa2, oc2, gpre, gpost, wgate, bgate, cw, cb, lng, lnb, wpa, wpb, wpc, wout):
    n = x2.shape[0]
    hpt = TOK_TILE // CONV_HALO

    def tok(width):
        return pl.BlockSpec((TOK_TILE, width), lambda i: (i, 0))

    return pl.pallas_call(
        _merge_body,
        grid=(n // TOK_TILE,),
        in_specs=[tok(D_MODEL), tok(CONV_CH),
                  pl.BlockSpec((CONV_HALO, CONV_CH), lambda i: (jnp.maximum(i * hpt - 1, 0), 0)),
                  tok(GROUP_W), tok(DIFF_WIDTH),
                  _resident((1, D_MODEL)), _resident((1, D_MODEL)),
                  _resident((D_MODEL, GATE_W)), _resident((1, GATE_W)),
                  _resident((CONV_WIDTH, CONV_CH)), _resident((1, CONV_CH)),
                  _resident((1, CONV_CH)), _resident((1, CONV_CH)),
                  _resident((GROUP_W, D_MODEL)), _resident((CONV_CH, D_MODEL)),
                  _resident((DIFF_WIDTH, D_MODEL)), _resident((D_MODEL, D_MODEL))],
        out_specs=tok(D_MODEL),
        out_shape=jax.ShapeDtypeStruct((n, D_MODEL), F32),
        scratch_shapes=[pltpu.VMEM((CONV_HALO + TOK_TILE, CONV_CH), F32)],
        compiler_params=_cparams(1),
        name="merge",
    )(x2, z2, z2, oa2, oc2, gpre, gpost, wgate, bgate, cw, cb, lng, lnb, wpa, wpb, wpc, wout)


def _rope_tables():
    inv = ROPE_THETA ** (-jnp.arange(0, HEAD_DIM, 2, dtype=F32) / HEAD_DIM)
    ang = jnp.arange(SEQ, dtype=F32)[:, None] * inv[None, :]
    cos, sin = jnp.cos(ang), jnp.sin(ang)
    reps = LANES // HEAD_DIM
    ct = jnp.tile(jnp.concatenate([cos, cos], axis=1), (1, reps))
    st = jnp.tile(jnp.concatenate([-sin, sin], axis=1), (1, reps))
    scale = HEAD_DIM ** -0.5
    return ct * scale, st * scale, ct, st


def _lambda_init(layer):
    return 0.8 - 0.6 * float(np.exp(-0.3 * layer))


def kernel(x, ffn1_norm_pre, ffn1_norm_post, ffn1_w_gate, ffn1_w_up, ffn1_w_down, mix_norm_pre, mix_norm_post, w_in, b_gate, conv_w, conv_b, conv_norm_g, conv_norm_b, lambda_q1, lambda_k1, lambda_q2, lambda_k2, diff_subln, w_proj_a, w_proj_b, w_proj_c, w_out, ffn2_norm_pre, ffn2_norm_post, ffn2_w_gate, ffn2_w_up, ffn2_w_down):
    batch, seq, d = x.shape
    assert seq == SEQ and d == D_MODEL and (batch * seq) % TOK_TILE == 0
    n = batch * seq
    rope_tabs = _rope_tables()
    row = lambda v: v.reshape(1, -1).astype(F32)
    x2 = x.reshape(n, d)
    for l in range(DEPTH):
        x2 = _ffn(x2, row(ffn1_norm_pre[l]), row(ffn1_norm_post[l]),
                  ffn1_w_gate[l].astype(BF16), ffn1_w_up[l].astype(BF16), ffn1_w_down[l].astype(BF16))

        w_l = w_in[l].astype(BF16)
        a0, a1, a2, c, z = _proj(x2, row(mix_norm_pre[l]), w_l[:, :PROJ_W], rope_tabs, batch)
        oa = _dil_attn(a0, a1, a2)
        lam_tab = jnp.concatenate(
            [row(lambda_q1[l]), row(lambda_k1[l]), row(lambda_q2[l]), row(lambda_k2[l]),
             jnp.full((1, HEAD_DIM), _lambda_init(l), F32), jnp.zeros((3, HEAD_DIM), F32)], axis=0)
        oc = _diff_attn(c, lam_tab, row(diff_subln[l]))
        x2 = _merge(x2, z.reshape(n, CONV_CH), oa.reshape(n, GROUP_W), oc.reshape(n, DIFF_WIDTH),
                    row(mix_norm_pre[l]), row(mix_norm_post[l]), w_l[:, PROJ_W:], row(b_gate[l]),
                    conv_w[l].astype(F32), row(conv_b[l]), row(conv_norm_g[l]), row(conv_norm_b[l]),
                    w_proj_a[l].astype(BF16), w_proj_b[l].astype(BF16), w_proj_c[l].astype(BF16),
                    w_out[l].astype(BF16))

        x2 = _ffn(x2, row(ffn2_norm_pre[l]), row(ffn2_norm_post[l]),
                  ffn2_w_gate[l].astype(BF16), ffn2_w_up[l].astype(BF16), ffn2_w_down[l].astype(BF16))
    return x2.reshape(batch, seq, d)
```

```python
import functools
import math

import numpy as np
import jax
import jax.numpy as jnp
from jax import lax
from jax.experimental import pallas as pl
from jax.experimental.pallas import tpu as pltpu

F32 = jnp.float32
BF16 = jnp.bfloat16

D_MODEL = 1024
SEQ = 2048
DEPTH = 4
HEAD_DIM = 64
ROPE_THETA = 10000.0
EPS = 1e-6
DIL_GROUPS = ((128, 1), (512, 4), (2048, 16))
N_GROUPS = len(DIL_GROUPS)
GROUP_W = 256
GROUP_HEADS = GROUP_W // HEAD_DIM
DIL_WIDTH = N_GROUPS * GROUP_W
CONV_CH = 512
CONV_WIDTH = 31
DIFF_HEADS = 4
DIFF_WIDTH = 512
D_FF = 2816
N_BRANCH = 3
PROJ_W = 3 * DIL_WIDTH + 3 * DIFF_WIDTH + 2 * CONV_CH
GATE_W = N_BRANCH * D_MODEL

LANES = 128
SUBLANES = 8
ATT_BLK = 128
DIFF_BLK = 256
TOK_TILE = 512
CONV_HALO = 32
NEG = -1e30
V7X_VMEM_LIMIT = 56 * 1024 * 1024


def _cparams(n_grid):
    return pltpu.CompilerParams(dimension_semantics=("arbitrary",) * n_grid,
                                vmem_limit_bytes=V7X_VMEM_LIMIT)


def _resident(shape):
    nd = len(shape)
    return pl.BlockSpec(shape, lambda *_: (0,) * nd, pipeline_mode=pl.Buffered(1))


def _rms(x, g):
    return x * lax.rsqrt(jnp.mean(x * x, axis=-1, keepdims=True) + EPS) * g


def _sigmoid(x):
    return 1.0 / (1.0 + jnp.exp(-x))


def _dot(a, b):
    return jnp.dot(a, b, preferred_element_type=F32)


def _dot_nt(a, b):
    return lax.dot_general(a, b, (((1,), (1,)), ((), ())), preferred_element_type=F32)


def _ffn_body(x_ref, gpre_ref, gpost_ref, wg_ref, wu_ref, wd_ref, o_ref):
    x = x_ref[...]
    h = _rms(x, gpre_ref[...]).astype(BF16)
    g = _dot(h, wg_ref[...])
    u = _dot(h, wu_ref[...])
    a = (g * _sigmoid(g) * u).astype(BF16)
    y = _dot(a, wd_ref[...])
    o_ref[...] = x + 0.5 * _rms(y, gpost_ref[...])


def _ffn(x2, gpre, gpost, wg, wu, wd):
    n = x2.shape[0]
    tok = pl.BlockSpec((TOK_TILE, D_MODEL), lambda i: (i, 0))
    return pl.pallas_call(
        _ffn_body,
        grid=(n // TOK_TILE,),
        in_specs=[tok, _resident((1, D_MODEL)), _resident((1, D_MODEL)),
                  _resident((D_MODEL, D_FF)), _resident((D_MODEL, D_FF)), _resident((D_FF, D_MODEL))],
        out_specs=tok,
        out_shape=jax.ShapeDtypeStruct((n, D_MODEL), F32),
        compiler_params=_cparams(1),
        name="ffn",
    )(x2, gpre, gpost, wg, wu, wd)


def _rope(r, ct, st):
    first_half = (lax.broadcasted_iota(jnp.int32, (1, LANES), 1) % HEAD_DIM) < (HEAD_DIM // 2)
    outs = []
    for j in range(r.shape[1] // LANES):
        xb = r[:, j * LANES:(j + 1) * LANES]
        rot = jnp.where(first_half,
                        pltpu.roll(xb, LANES - HEAD_DIM // 2, 1),
                        pltpu.roll(xb, HEAD_DIM // 2, 1))
        outs.append(xb * ct + rot * st)
    return outs[0] if len(outs) == 1 else jnp.concatenate(outs, axis=1)


def _causal_conv(zp_ref, zs_ref, cw_ref, cb_ref):
    sh_rows = zs_ref.shape[1]
    for b in range(SUBLANES - 1):
        zs_ref[b] = zp_ref[b + 1:b + 1 + sh_rows, :]
    off = CONV_HALO - (CONV_WIDTH - 1)
    y = jnp.zeros((TOK_TILE, zp_ref.shape[1]), F32) + cb_ref[...]
    for k in range(CONV_WIDTH):
        sh, al = (off + k) % SUBLANES, ((off + k) // SUBLANES) * SUBLANES
        if sh == 0:
            tap = zp_ref[al:al + TOK_TILE, :]
        else:
            tap = zs_ref[sh - 1, al:al + TOK_TILE, :]
        y = y + tap * cw_ref[k:k + 1, :]
    return y


def _conv_scratch():
    sh_rows = TOK_TILE + CONV_HALO - SUBLANES
    return [pltpu.VMEM((CONV_HALO + TOK_TILE, CONV_CH), F32),
            pltpu.VMEM((SUBLANES - 1, sh_rows, CONV_CH), F32)]


def _proj_body(x_ref, g_ref, w_ref, cq_ref, sq_ref, ck_ref, sk_ref, cw_ref, cb_ref, lng_ref, lnb_ref,
               a0_ref, a1_ref, a2_ref, c_ref, ob_ref, scr_ref, zp_ref, zs_ref):
    @pl.when(pl.program_id(0) % (SEQ // TOK_TILE) == 0)
    def _():
        zp_ref[TOK_TILE:TOK_TILE + CONV_HALO, :] = jnp.zeros((CONV_HALO, CONV_CH), F32)

    h = _rms(x_ref[...], g_ref[...]).astype(BF16)

    base = 3 * DIL_WIDTH + 3 * DIFF_WIDTH
    r = _dot(h, w_ref[:, base:base + 2 * CONV_CH])
    z = r[:, :CONV_CH] * _sigmoid(r[:, CONV_CH:])
    zp_ref[0:CONV_HALO, :] = zp_ref[TOK_TILE:TOK_TILE + CONV_HALO, :]
    zp_ref[CONV_HALO:, :] = z
    y = _causal_conv(zp_ref, zs_ref, cw_ref, cb_ref)
    mu = jnp.mean(y, axis=-1, keepdims=True)
    yc = y - mu
    var = jnp.mean(yc * yc, axis=-1, keepdims=True)
    yn = yc * lax.rsqrt(var + EPS) * lng_ref[...] + lnb_ref[...]
    ob_ref[0] = (yn * _sigmoid(yn)).astype(BF16)

    tables = ((cq_ref[...], sq_ref[...]), (ck_ref[...], sk_ref[...]))
    a_refs = (a0_ref, a1_ref, a2_ref)
    for part in range(3):
        for grp, (_, dil) in enumerate(DIL_GROUPS):
            c0 = part * DIL_WIDTH + grp * GROUP_W
            r = _dot(h, w_ref[:, c0:c0 + GROUP_W])
            if part < 2:
                r = _rope(r, *tables[part])
            oc0 = part * GROUP_W
            if dil == 1:
                a_refs[grp][0, :, oc0:oc0 + GROUP_W] = r.astype(BF16)
                continue
            rows = TOK_TILE // dil
            for slab in range(GROUP_W // LANES):
                si = ((grp - 1) * 3 + part) * (GROUP_W // LANES) + slab
                scr_ref[si] = r[:, slab * LANES:(slab + 1) * LANES]
                for res in range(dil):
                    a_refs[grp][0, res, :, oc0 + slab * LANES:oc0 + (slab + 1) * LANES] = (
                        scr_ref[si, pl.ds(res, rows, stride=dil), :].astype(BF16))
    base = 3 * DIL_WIDTH
    for part in range(3):
        r = _dot(h, w_ref[:, base + part * DIFF_WIDTH:base + (part + 1) * DIFF_WIDTH])
        if part < 2:
            r = _rope(r, *tables[part])
        c_ref[0, :, part * DIFF_WIDTH:(part + 1) * DIFF_WIDTH] = r.astype(BF16)


def _proj(x2, g, w, rope_tabs, cw, cb, lng, lnb, batch):
    n = x2.shape[0]
    tps = SEQ // TOK_TILE
    tab = pl.BlockSpec((TOK_TILE, LANES), lambda i: (i % tps, 0))
    n_scr = 2 * 3 * (GROUP_W // LANES)

    def seq_tile(width):
        return pl.BlockSpec((1, TOK_TILE, width), lambda i: (i // tps, i % tps, 0))

    return pl.pallas_call(
        _proj_body,
        grid=(n // TOK_TILE,),
        in_specs=[pl.BlockSpec((TOK_TILE, D_MODEL), lambda i: (i, 0)),
                  _resident((1, D_MODEL)), _resident((D_MODEL, PROJ_W)), tab, tab, tab, tab,
                  _resident((CONV_WIDTH, CONV_CH)), _resident((1, CONV_CH)),
                  _resident((1, CONV_CH)), _resident((1, CONV_CH))],
        out_specs=[
            seq_tile(3 * GROUP_W),
            pl.BlockSpec((1, 4, TOK_TILE // 4, 3 * GROUP_W), lambda i: (i // tps, 0, i % tps, 0)),
            pl.BlockSpec((1, 16, TOK_TILE // 16, 3 * GROUP_W), lambda i: (i // tps, 0, i % tps, 0)),
            seq_tile(3 * DIFF_WIDTH), seq_tile(CONV_CH),
        ],
        out_shape=[
            jax.ShapeDtypeStruct((batch, SEQ, 3 * GROUP_W), BF16),
            jax.ShapeDtypeStruct((batch, 4, SEQ // 4, 3 * GROUP_W), BF16),
            jax.ShapeDtypeStruct((batch, 16, SEQ // 16, 3 * GROUP_W), BF16),
            jax.ShapeDtypeStruct((batch, SEQ, 3 * DIFF_WIDTH), BF16),
            jax.ShapeDtypeStruct((batch, SEQ, CONV_CH), BF16),
        ],
        scratch_shapes=[pltpu.VMEM((n_scr, TOK_TILE, LANES), F32)] + _conv_scratch(),
        compiler_params=_cparams(1),
        name="in_proj",
    )(x2, g, w, *rope_tabs, cw, cb, lng, lnb)


def _dil_block(q, kcat, vcat, mask4):
    nq = q.shape[0]
    lane_head = lax.broadcasted_iota(jnp.int32, (1, GROUP_W), 1) // HEAD_DIM
    hm = [jnp.where(lane_head == h, 1.0, 0.0).astype(BF16) for h in range(GROUP_HEADS)]
    qs = jnp.concatenate([q * hm[h] for h in range(GROUP_HEADS)], axis=0)
    s = jnp.where(mask4, _dot_nt(qs, kcat), NEG)
    m = jnp.max(s, axis=1, keepdims=True)
    p = jnp.exp2(s - m)
    l = jnp.sum(p, axis=1, keepdims=True)
    r = _dot(p.astype(BF16), vcat) * (1.0 / l)
    t = m + jnp.log2(l)
    o = r[0:nq]
    lse = jnp.broadcast_to(t[0:nq], (nq, GROUP_W))
    for h in range(1, GROUP_HEADS):
        sel = lane_head == h
        o = jnp.where(sel, r[h * nq:(h + 1) * nq], o)
        lse = jnp.where(sel, t[h * nq:(h + 1) * nq], lse)
    return o, lse


def _dil_attn_body(a0_ref, a1_ref, a2_ref, o_ref, acc_ref, lse_ref):
    n_blk = SEQ // ATT_BLK
    qi = lax.broadcasted_iota(jnp.int32, (GROUP_HEADS * ATT_BLK, 2 * ATT_BLK), 0) % ATT_BLK
    kj = lax.broadcasted_iota(jnp.int32, (GROUP_HEADS * ATT_BLK, 2 * ATT_BLK), 1)
    band4 = ((kj < ATT_BLK) & (kj >= qi)) | ((kj >= ATT_BLK) & (kj - ATT_BLK <= qi))
    diag4 = (lax.broadcasted_iota(jnp.int32, (GROUP_HEADS * ATT_BLK, ATT_BLK), 1)
             <= lax.broadcasted_iota(jnp.int32, (GROUP_HEADS * ATT_BLK, ATT_BLK), 0) % ATT_BLK)
    qc, kc, vc = (slice(p * GROUP_W, (p + 1) * GROUP_W) for p in range(3))

    def store(grp, start, dil, o, lse):
        for slab in range(GROUP_W // LANES):
            sl = slice(slab * LANES, (slab + 1) * LANES)
            idx = pl.ds(start, ATT_BLK) if dil == 1 else pl.ds(start, ATT_BLK, stride=dil)
            acc_ref[grp, slab, idx, :] = o[:, sl]
            lse_ref[grp, slab, idx, :] = lse[:, sl]

    def first_block(read):
        return _dil_block(read(0, qc), read(0, kc), read(0, vc), diag4)

    def later_block(read, cur):
        rows = 2 * ATT_BLK
        prv = cur - ATT_BLK if isinstance(cur, int) else pl.multiple_of(cur - ATT_BLK, ATT_BLK)
        return _dil_block(read(cur, qc), read(prv, kc, rows), read(prv, vc, rows), band4)

    def read0(start, cols, rows=ATT_BLK):
        return a0_ref[0, pl.ds(start, rows), cols]

    store(0, 0, 1, *first_block(read0))
    per_it = 5

    def g0(it, carry):
        for u in range(per_it):
            cur = pl.multiple_of((it * per_it + u + 1) * ATT_BLK, ATT_BLK)
            store(0, cur, 1, *later_block(read0, cur))
        return carry

    lax.fori_loop(0, (n_blk - 1) // per_it, g0, 0)

    sub_blk = (SEQ // 4) // ATT_BLK

    res_per_it = 2

    def g1(it, carry):
        for u in range(res_per_it):
            res = it * res_per_it + u

            def read1(start, cols, rows=ATT_BLK):
                return a1_ref[0, res, pl.ds(start, rows), cols]

            store(1, res, 4, *first_block(read1))
            for jb in range(1, sub_blk):
                store(1, jb * ATT_BLK * 4 + res, 4, *later_block(read1, jb * ATT_BLK))
        return carry

    lax.fori_loop(0, 4 // res_per_it, g1, 0)

    per_it2 = 8

    def g2(it, carry):
        for u in range(per_it2):
            res = it * per_it2 + u

            def read2(start, cols, rows=ATT_BLK):
                return a2_ref[0, res, pl.ds(start, rows), cols]

            store(2, res, 16, *first_block(read2))
        return carry

    lax.fori_loop(0, n_blk // per_it2, g2, 0)

    rows_per = 256

    def combine(ci, carry):
        rows = pl.ds(pl.multiple_of(ci * rows_per, rows_per), rows_per)
        for slab in range(GROUP_W // LANES):
            ls = [lse_ref[g, slab, rows, :] for g in range(N_GROUPS)]
            mx = jnp.maximum(jnp.maximum(ls[0], ls[1]), ls[2])
            ws = [jnp.exp2(l - mx) for l in ls]
            num = ws[0] * acc_ref[0, slab, rows, :]
            for g in range(1, N_GROUPS):
                num = num + ws[g] * acc_ref[g, slab, rows, :]
            o_ref[0, rows, slab * LANES:(slab + 1) * LANES] = (num / (ws[0] + ws[1] + ws[2])).astype(BF16)
        return carry

    lax.fori_loop(0, SEQ // rows_per, combine, 0)


def _dil_attn(a0, a1, a2):
    batch = a0.shape[0]
    return pl.pallas_call(
        _dil_attn_body,
        grid=(batch,),
        in_specs=[pl.BlockSpec((1, SEQ, 3 * GROUP_W), lambda b: (b, 0, 0)),
                  pl.BlockSpec((1, 4, SEQ // 4, 3 * GROUP_W), lambda b: (b, 0, 0, 0)),
                  pl.BlockSpec((1, 16, SEQ // 16, 3 * GROUP_W), lambda b: (b, 0, 0, 0))],
        out_specs=pl.BlockSpec((1, SEQ, GROUP_W), lambda b: (b, 0, 0)),
        out_shape=jax.ShapeDtypeStruct((batch, SEQ, GROUP_W), BF16),
        scratch_shapes=[pltpu.VMEM((N_GROUPS, GROUP_W // LANES, SEQ, LANES), F32),
                        pltpu.VMEM((N_GROUPS, GROUP_W // LANES, SEQ, LANES), F32)],
        compiler_params=_cparams(1),
        name="dil_attn",
    )(a0, a1, a2)


def _diff_attn_body(c_ref, lam_ref, subln_ref, o_ref, q_scr, acc_ref):
    hw = 2 * HEAD_DIM
    pair = 2 * DIFF_BLK
    n_rows = 2 * pair
    lam = (jnp.exp(jnp.sum(lam_ref[0:1, :] * lam_ref[1:2, :], axis=1, keepdims=True))
           - jnp.exp(jnp.sum(lam_ref[2:3, :] * lam_ref[3:4, :], axis=1, keepdims=True))
           + lam_ref[4:5, 0:1])
    out_scale = 1.0 - lam_ref[4:5, 0:1]
    lane = lax.broadcasted_iota(jnp.int32, (1, hw), 1)
    map1 = jnp.where(lane < HEAD_DIM, 1.0, 0.0).astype(BF16)
    map2 = jnp.where(lane >= HEAD_DIM, 1.0, 0.0).astype(BF16)
    row = lax.broadcasted_iota(jnp.int32, (n_rows, DIFF_BLK), 0)
    col = lax.broadcasted_iota(jnp.int32, (n_rows, DIFF_BLK), 1)
    mask_a = (row >= pair) | (col <= row % DIFF_BLK)
    mask_b = (col <= row % DIFF_BLK)[:pair]

    def kv_step(ms, kstart, nk, r0, mask):
        nr = n_rows - r0
        ones = jnp.ones((nk, hw), BF16)
        new_ms = []
        for h in range(DIFF_HEADS):
            k = c_ref[0, pl.ds(kstart, nk), DIFF_WIDTH + h * hw:DIFF_WIDTH + (h + 1) * hw]
            v = c_ref[0, pl.ds(kstart, nk), 2 * DIFF_WIDTH + h * hw:2 * DIFF_WIDTH + (h + 1) * hw]
            s = _dot_nt(q_scr[h, r0:, :], k)
            if mask is not None:
                s = jnp.where(mask, s, NEG)
            m_old = ms[h][r0:]
            m_new = jnp.maximum(m_old, jnp.max(s, axis=1, keepdims=True))
            alpha = jnp.exp2(m_old - m_new)
            p = jnp.exp2(s - m_new).astype(BF16)
            pv = _dot(p, jnp.concatenate([v, ones], axis=1))
            acc_ref[h, r0:, :] = acc_ref[h, r0:, :] * alpha + pv
            new_ms.append(m_new if r0 == 0 else jnp.concatenate([ms[h][:r0], m_new], axis=0))
        return tuple(new_ms)

    def q_pair(j, carry):
        base = pl.multiple_of(j * pair, pair)
        for h in range(DIFF_HEADS):
            q = c_ref[0, pl.ds(base, pair), h * hw:(h + 1) * hw]
            qa, qb = q[:DIFF_BLK], q[DIFF_BLK:]
            q_scr[h] = jnp.concatenate([qa * map1, qa * map2, qb * map1, qb * map2], axis=0)
        acc_ref[...] = jnp.zeros_like(acc_ref)
        ms = tuple(jnp.full((n_rows, 1), NEG, F32) for _ in range(DIFF_HEADS))
        ms = lax.fori_loop(
            0, j, lambda kb, ms: kv_step(ms, pl.multiple_of(kb * pair, pair), pair, 0, None), ms)
        ms = kv_step(ms, base, DIFF_BLK, 0, mask_a)
        kv_step(ms, pl.multiple_of(base + DIFF_BLK, DIFF_BLK), DIFF_BLK, pair, mask_b)

        for h in range(DIFF_HEADS):
            for blk in range(2):
                a1 = acc_ref[h, (2 * blk) * DIFF_BLK:(2 * blk + 1) * DIFF_BLK, :]
                a2 = acc_ref[h, (2 * blk + 1) * DIFF_BLK:(2 * blk + 2) * DIFF_BLK, :]
                o = a1[:, :hw] / a1[:, hw:] - lam * (a2[:, :hw] / a2[:, hw:])
                o = _rms(o, subln_ref[...]) * out_scale
                o_ref[0, pl.ds(base + blk * DIFF_BLK, DIFF_BLK), h * hw:(h + 1) * hw] = o.astype(BF16)
        return carry

    lax.fori_loop(0, SEQ // pair, q_pair, 0)


def _diff_attn(c, lam_tab, subln):
    batch = c.shape[0]
    hw = 2 * HEAD_DIM
    return pl.pallas_call(
        _diff_attn_body,
        grid=(batch,),
        in_specs=[pl.BlockSpec((1, SEQ, 3 * DIFF_WIDTH), lambda b: (b, 0, 0)),
                  _resident((8, HEAD_DIM)), _resident((1, hw))],
        out_specs=pl.BlockSpec((1, SEQ, DIFF_WIDTH), lambda b: (b, 0, 0)),
        out_shape=jax.ShapeDtypeStruct((batch, SEQ, DIFF_WIDTH), BF16),
        scratch_shapes=[pltpu.VMEM((DIFF_HEADS, 4 * DIFF_BLK, hw), BF16),
                        pltpu.VMEM((DIFF_HEADS, 4 * DIFF_BLK, 2 * hw), F32)],
        compiler_params=_cparams(1),
        name="diff_attn",
    )(c, lam_tab, subln)


def _merge_body(x_ref, oa_ref, ob_ref, oc_ref, gpre_ref, gpost_ref, wgate_ref, bgate_ref,
                wpa_ref, wpb_ref, wpc_ref, wout_ref, o_ref):
    x = x_ref[...]
    h = _rms(x, gpre_ref[...]).astype(BF16)
    branches = ((oa_ref, wpa_ref), (ob_ref, wpb_ref), (oc_ref, wpc_ref))
    merged = None
    for j, (act_ref, w_ref) in enumerate(branches):
        gate = _sigmoid(_dot(h, wgate_ref[:, j * D_MODEL:(j + 1) * D_MODEL])
                        + bgate_ref[:, j * D_MODEL:(j + 1) * D_MODEL])
        term = gate * _dot(act_ref[...], w_ref[...])
        merged = term if merged is None else merged + term
    y_out = _dot(merged.astype(BF16), wout_ref[...])
    o_ref[...] = x + _rms(y_out, gpost_ref[...])


def _merge(x2, oa2, ob2, oc2, gpre, gpost, wgate, bgate, wpa, wpb, wpc, wout):
    n = x2.shape[0]

    def tok(width):
        return pl.BlockSpec((TOK_TILE, width), lambda i: (i, 0))

    return pl.pallas_call(
        _merge_body,
        grid=(n // TOK_TILE,),
        in_specs=[tok(D_MODEL), tok(GROUP_W), tok(CONV_CH), tok(DIFF_WIDTH),
                  _resident((1, D_MODEL)), _resident((1, D_MODEL)),
                  _resident((D_MODEL, GATE_W)), _resident((1, GATE_W)),
                  _resident((GROUP_W, D_MODEL)), _resident((CONV_CH, D_MODEL)),
                  _resident((DIFF_WIDTH, D_MODEL)), _resident((D_MODEL, D_MODEL))],
        out_specs=tok(D_MODEL),
        out_shape=jax.ShapeDtypeStruct((n, D_MODEL), F32),
        compiler_params=_cparams(1),
        name="merge",
    )(x2, oa2, ob2, oc2, gpre, gpost, wgate, bgate, wpa, wpb, wpc, wout)


def _rope_tables():
    inv = ROPE_THETA ** (-jnp.arange(0, HEAD_DIM, 2, dtype=F32) / HEAD_DIM)
    ang = jnp.arange(SEQ, dtype=F32)[:, None] * inv[None, :]
    cos, sin = jnp.cos(ang), jnp.sin(ang)
    reps = LANES // HEAD_DIM
    ct = jnp.tile(jnp.concatenate([cos, cos], axis=1), (1, reps))
    st = jnp.tile(jnp.concatenate([-sin, sin], axis=1), (1, reps))
    scale = (HEAD_DIM ** -0.5) * math.log2(math.e)
    return ct * scale, st * scale, ct, st


def _lambda_init(layer):
    return 0.8 - 0.6 * float(np.exp(-0.3 * layer))


def kernel(x, ffn1_norm_pre, ffn1_norm_post, ffn1_w_gate, ffn1_w_up, ffn1_w_down, mix_norm_pre, mix_norm_post, w_in, b_gate, conv_w, conv_b, conv_norm_g, conv_norm_b, lambda_q1, lambda_k1, lambda_q2, lambda_k2, diff_subln, w_proj_a, w_proj_b, w_proj_c, w_out, ffn2_norm_pre, ffn2_norm_post, ffn2_w_gate, ffn2_w_up, ffn2_w_down):
    batch, seq, d = x.shape
    assert seq == SEQ and d == D_MODEL and (batch * seq) % TOK_TILE == 0
    n = batch * seq
    rope_tabs = _rope_tables()
    row = lambda v: v.reshape(1, -1).astype(F32)
    x2 = x.reshape(n, d)
    for l in range(DEPTH):
        x2 = _ffn(x2, row(ffn1_norm_pre[l]), row(ffn1_norm_post[l]),
                  ffn1_w_gate[l].astype(BF16), ffn1_w_up[l].astype(BF16), ffn1_w_down[l].astype(BF16))

        w_l = w_in[l].astype(BF16)
        a0, a1, a2, c, ob = _proj(x2, row(mix_norm_pre[l]), w_l[:, :PROJ_W], rope_tabs,
                                  conv_w[l].astype(F32), row(conv_b[l]), row(conv_norm_g[l]),
                                  row(conv_norm_b[l]), batch)
        oa = _dil_attn(a0, a1, a2)
        lam_tab = jnp.concatenate(
            [row(lambda_q1[l]), row(lambda_k1[l]), row(lambda_q2[l]), row(lambda_k2[l]),
             jnp.full((1, HEAD_DIM), _lambda_init(l), F32), jnp.zeros((3, HEAD_DIM), F32)], axis=0)
        oc = _diff_attn(c, lam_tab, row(diff_subln[l]))
        x2 = _merge(x2, oa.reshape(n, GROUP_W), ob.reshape(n, CONV_CH), oc.reshape(n, DIFF_WIDTH),
                    row(mix_norm_pre[l]), row(mix_norm_post[l]), w_l[:, PROJ_W:], row(b_gate[l]),
                    w_proj_a[l].astype(BF16), w_proj_b[l].astype(BF16), w_proj_c[l].astype(BF16),
                    w_out[l].astype(BF16))

        x2 = _ffn(x2, row(ffn2_norm_pre[l]), row(ffn2_norm_post[l]),
                  ffn2_w_gate[l].astype(BF16), ffn2_w_up[l].astype(BF16), ffn2_w_down[l].astype(BF16))
    return x2.reshape(batch, seq, d)
```
